```python
import math
import jax, jax.numpy as jnp
from jax import lax
import numpy as np

D_MODEL = 1024
BATCH = 16
SEQ = 4096
DEPTH = 1
DEC_BATCH = 2
DEC_SEQ = 8192
PAST_LEN = 128

D_RNN = D_MODEL
N_LRU_HEADS = 8
LRU_BLOCK = D_RNN // N_LRU_HEADS
LRU_C = 8.0
CONV_A_WIDTH = 4
CONV_A_LEFT = 2
D_FNET = D_MODEL // 2
N_FNET_GROUPS = 4
FNET_GROUP = D_FNET // N_FNET_GROUPS
D_FF = 2816
CONV_F_WIDTH = 3
CONV_F_LEFT = 1
N_MOD = 6
RMS_EPS = 1e-6
D_IN = 2 * D_RNN + D_FNET + 2 * D_MODEL

kernel_name = "hybrid_rglru_fnet_convffn_encoder"


def rms_norm(x, g):
    xf = x.astype(jnp.float32)
    y = xf * lax.rsqrt(jnp.mean(xf * xf, axis=-1, keepdims=True) + RMS_EPS)
    return (y * g.astype(jnp.float32)).astype(x.dtype)


def depthwise_conv(x, w, b, left):
    k_w = w.shape[0]
    s = x.shape[1]
    xp = jnp.pad(x, ((0, 0), (left, k_w - 1 - left), (0, 0)))
    out = b + xp[:, 0:s] * w[0]
    for k in range(1, k_w):
        out = out + xp[:, k:k + s] * w[k]
    return out


def _lru_combine(left, right):
    a_l, b_l = left
    a_r, b_r = right
    return (a_l * a_r, a_r * b_l + b_r)


def rg_lru(x, w_a, b_a, w_x, b_x, lam, reverse):
    bsz, s, w = x.shape
    xh = x.reshape(bsz, s, N_LRU_HEADS, LRU_BLOCK)
    r = jax.nn.sigmoid((jnp.einsum('bshi,hij->bshj', xh, w_a).reshape(bsz, s, w) + b_a).astype(jnp.float32))
    i = jax.nn.sigmoid((jnp.einsum('bshi,hij->bshj', xh, w_x).reshape(bsz, s, w) + b_x).astype(jnp.float32))
    log_a = -LRU_C * r * jax.nn.softplus(-lam.astype(jnp.float32))
    a = jnp.exp(log_a)
    mult = jnp.sqrt(jnp.maximum(-jnp.expm1(2.0 * log_a), 0.0))
    b = mult * i * x.astype(jnp.float32)
    _, h = lax.associative_scan(_lru_combine, (a, b), reverse=reverse, axis=1)
    return h.astype(x.dtype)


def fourier_mix(x):
    bsz, s, _ = x.shape
    xg = x.reshape(bsz, s, N_FNET_GROUPS, FNET_GROUP).astype(jnp.float32)
    f = jnp.fft.fft2(xg, axes=(1, 3), norm='ortho').real
    return f.reshape(bsz, s, D_FNET).astype(x.dtype)


def encoder_layer(x, c, w_ada, b_ada, g_pre_mix, w_in, conv_w, conv_b,
                  w_lru_a, b_lru_a, w_lru_x, b_lru_x, lru_lambda,
                  w_a_out, w_b_out, w_o, g_post_mix,
                  g_pre_ffn, w_up, ffn_conv_w, ffn_conv_b, w_down, g_post_ffn):
    mod = jax.nn.silu(c) @ w_ada + b_ada
    sh1, sc1, gt1, sh2, sc2, gt2 = [m[:, None, :] for m in jnp.split(mod, N_MOD, axis=-1)]

    h = rms_norm(x, g_pre_mix) * (1.0 + sc1) + sh1
    u = h @ w_in
    o1 = D_RNN
    o2 = o1 + D_RNN
    o3 = o2 + D_FNET
    o4 = o3 + D_MODEL
    xa, ya, xb, ga, gb = u[..., :o1], u[..., o1:o2], u[..., o2:o3], u[..., o3:o4], u[..., o4:]

    xa = depthwise_conv(xa, conv_w, conv_b, CONV_A_LEFT)
    h_fwd = rg_lru(xa, w_lru_a[0], b_lru_a[0], w_lru_x[0], b_lru_x[0], lru_lambda[0], reverse=False)
    h_bwd = rg_lru(xa, w_lru_a[1], b_lru_a[1], w_lru_x[1], b_lru_x[1], lru_lambda[1], reverse=True)
    y_a = ((h_fwd + h_bwd) * jax.nn.gelu(ya)) @ w_a_out

    y_b = fourier_mix(xb) @ w_b_out

    merged = jax.nn.sigmoid(ga) * y_a + jax.nn.sigmoid(gb) * y_b
    x = x + gt1 * rms_norm(merged @ w_o, g_post_mix)

    h = rms_norm(x, g_pre_ffn) * (1.0 + sc2) + sh2
    up = depthwise_conv(h @ w_up, ffn_conv_w, ffn_conv_b, CONV_F_LEFT)
    val, gate = up[..., :D_FF], up[..., D_FF:]
    f = (jax.nn.gelu(gate) * val) @ w_down
    x = x + gt2 * rms_norm(f, g_post_ffn)
    return x


def setup_inputs(seed: int = 0) -> dict:
    key = jax.random.key(seed)
    ks = jax.random.split(key, 32)
    f32 = jnp.float32
    nrm = lambda k, shape, scale: jax.random.normal(k, shape, f32) * scale
    p = jax.random.uniform(ks[13], (DEPTH, 2, D_RNN), f32, 0.9, 0.999)
    return {
        'x_prompt': nrm(ks[0], (BATCH, SEQ, D_MODEL), 1.0),
        'x_sample': nrm(ks[1], (DEC_BATCH, DEC_SEQ, D_MODEL), 1.0),
        'c_prompt': nrm(ks[2], (BATCH, D_MODEL), 1.0),
        'c_sample': nrm(ks[3], (DEC_BATCH, D_MODEL), 1.0),
        'w_ada': nrm(ks[4], (DEPTH, D_MODEL, N_MOD * D_MODEL), D_MODEL ** -0.5),
        'b_ada': nrm(ks[5], (DEPTH, N_MOD * D_MODEL), 0.02),
        'g_pre_mix': 1.0 + nrm(ks[6], (DEPTH, D_MODEL), 0.02),
        'w_in': nrm(ks[7], (DEPTH, D_MODEL, D_IN), D_MODEL ** -0.5),
        'conv_w': nrm(ks[8], (DEPTH, CONV_A_WIDTH, D_RNN), CONV_A_WIDTH ** -0.5),
        'conv_b': nrm(ks[9], (DEPTH, D_RNN), 0.02),
        'w_lru_a': nrm(ks[10], (DEPTH, 2, N_LRU_HEADS, LRU_BLOCK, LRU_BLOCK), LRU_BLOCK ** -0.5),
        'b_lru_a': nrm(ks[11], (DEPTH, 2, D_RNN), 0.02),
        'w_lru_x': nrm(ks[12], (DEPTH, 2, N_LRU_HEADS, LRU_BLOCK, LRU_BLOCK), LRU_BLOCK ** -0.5),
        'b_lru_x': nrm(ks[14], (DEPTH, 2, D_RNN), 0.02),
        'lru_lambda': jnp.log(p) - jnp.log1p(-p),
        'w_a_out': nrm(ks[15], (DEPTH, D_RNN, D_MODEL), D_RNN ** -0.5),
        'w_b_out': nrm(ks[16], (DEPTH, D_FNET, D_MODEL), D_FNET ** -0.5),
        'w_o': nrm(ks[17], (DEPTH, D_MODEL, D_MODEL), D_MODEL ** -0.5),
        'g_post_mix': 1.0 + nrm(ks[18], (DEPTH, D_MODEL), 0.02),
        'g_pre_ffn': 1.0 + nrm(ks[19], (DEPTH, D_MODEL), 0.02),
        'w_up': nrm(ks[20], (DEPTH, D_MODEL, 2 * D_FF), D_MODEL ** -0.5),
        'ffn_conv_w': nrm(ks[21], (DEPTH, CONV_F_WIDTH, 2 * D_FF), CONV_F_WIDTH ** -0.5),
        'ffn_conv_b': nrm(ks[22], (DEPTH, 2 * D_FF), 0.02),
        'w_down': nrm(ks[23], (DEPTH, D_FF, D_MODEL), D_FF ** -0.5),
        'g_post_ffn': 1.0 + nrm(ks[24], (DEPTH, D_MODEL), 0.02),
    }


def reference(x_prompt, x_sample, c_prompt, c_sample, w_ada, b_ada, g_pre_mix, w_in, conv_w, conv_b,
              w_lru_a, b_lru_a, w_lru_x, b_lru_x, lru_lambda, w_a_out, w_b_out, w_o, g_post_mix,
              g_pre_ffn, w_up, ffn_conv_w, ffn_conv_b, w_down, g_post_ffn):
    y_prompt = x_prompt
    y_sample = x_sample
    for l in range(DEPTH):
        layer_params = (w_ada[l], b_ada[l], g_pre_mix[l], w_in[l], conv_w[l], conv_b[l],
                        w_lru_a[l], b_lru_a[l], w_lru_x[l], b_lru_x[l], lru_lambda[l],
                        w_a_out[l], w_b_out[l], w_o[l], g_post_mix[l],
                        g_pre_ffn[l], w_up[l], ffn_conv_w[l], ffn_conv_b[l], w_down[l], g_post_ffn[l])
        y_prompt = encoder_layer(y_prompt, c_prompt, *layer_params)
        y_sample = encoder_layer(y_sample, c_sample, *layer_params)
    return (y_prompt, y_sample)
```

```python
import functools
import math

import jax
import jax.numpy as jnp
from jax import lax
from jax.experimental import pallas as pl
from jax.experimental.pallas import tpu as pltpu

F32 = jnp.float32
BF16 = jnp.bfloat16

LANES = 128
SUBLANES = 8
BF16_ROWS = 16

N_HEADS = 8
N_GROUPS = 4
N_MOD = 6
LRU_C = 8.0
RMS_EPS = 1e-6
GELU_K = math.sqrt(2.0 / math.pi)

LRU_SUB = 52
FFN_CHUNK = 256
VMEM_LIMIT = 56 * 1024 * 1024


def _gelu(x):
    return 0.5 * x * (1.0 + jnp.tanh(GELU_K * (x + 0.044715 * (x * x * x))))


def _sigmoid(x):
    return 0.5 * (1.0 + jnp.tanh(0.5 * x))


def _rms(x, g):
    return x * lax.rsqrt(jnp.mean(x * x, axis=-1, keepdims=True) + RMS_EPS) * g


def _const_spec(shape):
    nd = len(shape)
    return pl.BlockSpec(shape, lambda *_: (0,) * nd, pipeline_mode=pl.Buffered(1))


def _mod_kernel(c_ref, w_ref, b_ref, o_ref):
    c = c_ref[...]
    s = c * _sigmoid(c)
    o_ref[...] = jnp.dot(s, w_ref[...], preferred_element_type=F32) + b_ref[...]


def _modulation(c, w_ada, b_ada):
    nb, d = c.shape
    n = w_ada.shape[1]
    tn = 512
    return pl.pallas_call(
        _mod_kernel,
        grid=(n // tn,),
        in_specs=[pl.BlockSpec((nb, d), lambda j: (0, 0)),
                  pl.BlockSpec((d, tn), lambda j: (0, j)),
                  pl.BlockSpec((1, tn), lambda j: (0, j))],
        out_specs=pl.BlockSpec((nb, tn), lambda j: (0, j)),
        out_shape=jax.ShapeDtypeStruct((nb, n), F32),
        name="mod",
    )(c, w_ada, b_ada.reshape(1, n))


def _inproj_kernel(x_ref, mod_ref, g_ref, w_ref, dft_ref, xa_ref, yag_ref, pq_ref):
    d = x_ref.shape[-1]
    x = x_ref[0]
    m = mod_ref[0]
    h = _rms(x, g_ref[...]) * (1.0 + m[1:2]) + m[0:1]
    hb = h.astype(BF16)
    xa = jnp.dot(hb, w_ref[:, 0:d], preferred_element_type=F32)
    for hh in range(N_HEADS):
        xa_ref[0, hh] = xa[:, hh * LANES:(hh + 1) * LANES].astype(BF16)
    yag_ref[0] = jnp.dot(hb, w_ref[:, d:4 * d], preferred_element_type=F32).astype(BF16)
    xb = jnp.dot(hb, w_ref[:, 4 * d:], preferred_element_type=F32).astype(BF16)
    half = N_GROUPS * LANES
    for g in range(N_GROUPS):
        pq = jnp.dot(xb[:, g * LANES:(g + 1) * LANES], dft_ref[...], preferred_element_type=F32)
        pq_ref[0, :, g * LANES:(g + 1) * LANES] = pq[:, :LANES].astype(BF16)
        pq_ref[0, :, half + g * LANES:half + (g + 1) * LANES] = pq[:, LANES:].astype(BF16)


def _inproj(x, mod, g_pre, w_in_r, dft_g, ts):
    b, s, d = x.shape
    n = w_in_r.shape[1]
    return pl.pallas_call(
        _inproj_kernel,
        grid=(b, s // ts),
        in_specs=[pl.BlockSpec((1, ts, d), lambda i, j: (i, j, 0)),
                  pl.BlockSpec((1, N_MOD, d), lambda i, j: (i, 0, 0)),
                  _const_spec((1, d)),
                  _const_spec((d, n)),
                  _const_spec(dft_g.shape)],
        out_specs=[pl.BlockSpec((1, N_HEADS, ts, LANES), lambda i, j: (i, 0, j, 0)),
                   pl.BlockSpec((1, ts, 3 * d), lambda i, j: (i, j, 0)),
                   pl.BlockSpec((1, ts, 2 * N_GROUPS * LANES), lambda i, j: (i, j, 0))],
        out_shape=[jax.ShapeDtypeStruct((b, N_HEADS, s, LANES), BF16),
                   jax.ShapeDtypeStruct((b, s, 3 * d), BF16),
                   jax.ShapeDtypeStruct((b, s, 2 * N_GROUPS * LANES), BF16)],
        compiler_params=pltpu.CompilerParams(
            dimension_semantics=("parallel", "parallel"), vmem_limit_bytes=VMEM_LIMIT),
        name="inproj",
    )(x, mod, g_pre, w_in_r, dft_g)


def _lru_kernel(xa_ref, cw_ref, cb_ref, wg_ref, bg_ref, lam_ref, out_ref, xp_ref, hf_ref,
                *, seq, sub, n_chunks):
    tc = SUBLANES * sub
    sp = n_chunks * tc
    pad = SUBLANES

    xp_ref[0:pad, :] = jnp.zeros((pad, LANES), F32)
    xp_ref[pad:pad + seq, :] = xa_ref[0, 0].astype(F32)
    xp_ref[pad + seq:, :] = jnp.zeros((sp + pad - seq, LANES), F32)

    rid = lax.broadcasted_iota(jnp.int32, (SUBLANES, LANES), 0)
    row = lax.broadcasted_iota(jnp.int32, (tc, LANES), 0)
    t_perm = (row % SUBLANES) * sub + row // SUBLANES
    cw = cw_ref[...]
    cb = cb_ref[...]

    def chunk_inputs(t0, direction):
        base = pad + t0 - 2
        taps = [xp_ref[pl.ds(base + m, SUBLANES, stride=sub), :] for m in range(sub + 3)]
        pieces = []
        for j in range(sub):
            acc = cb + cw[0:1] * taps[j]
            for k in range(1, 4):
                acc = acc + cw[k:k + 1] * taps[j + k]
            pieces.append(acc)
        xc = jnp.concatenate(pieces, axis=0)
        pre = jnp.dot(xc.astype(BF16), wg_ref[direction, 0], preferred_element_type=F32)
        pre = pre + bg_ref[direction, 0]
        r = _sigmoid(pre[:, :LANES])
        i = _sigmoid(pre[:, LANES:])
        lam = lam_ref[direction, 0]
        softplus_neg = jnp.maximum(-lam, 0.0) + jnp.log1p(jnp.exp(-jnp.abs(lam)))
        a = jnp.exp((-LRU_C) * r * softplus_neg)
        mult = jnp.sqrt(jnp.maximum(1.0 - a * a, 0.0))
        return a, mult * i * xc

    def local_scan(a, b, order):
        h = jnp.zeros((SUBLANES, LANES), F32)
        p = jnp.ones((SUBLANES, LANES), F32)
        hs = [None] * sub
        ps = [None] * sub
        for j in order:
            aj = a[j * SUBLANES:(j + 1) * SUBLANES]
            h = aj * h + b[j * SUBLANES:(j + 1) * SUBLANES]
            p = aj * p
            hs[j] = h
            ps[j] = p
        return hs, ps, h, p

    def fwd_body(c, carry):
        t0 = c * tc
        a, b = chunk_inputs(t0, 0)
        hs, ps, e, p = local_scan(a, b, range(sub))
        for dlt in (1, 2, 4):
            keep = rid >= dlt
            es = jnp.where(keep, pltpu.roll(e, dlt, 0), 0.0)
            psh = jnp.where(keep, pltpu.roll(p, dlt, 0), 1.0)
            e = p * es + e
            p = p * psh
        full = e + p * carry
        cin = jnp.where(rid >= 1, pltpu.roll(full, 1, 0), carry)
        for j in range(sub):
            hf_ref[pl.ds(t0 + j * SUBLANES, SUBLANES), :] = hs[j] + ps[j] * cin
        return full[SUBLANES - 1:SUBLANES]

    def bwd_body(cc, carry):
        c = n_chunks - 1 - cc
        t0 = c * tc
        a, b = chunk_inputs(t0, 1)
        b = jnp.where(t_perm < seq - t0, b, 0.0)
        hs, ps, e, p = local_scan(a, b, range(sub - 1, -1, -1))
        for dlt in (1, 2, 4):
            keep = rid < SUBLANES - dlt
            es = jnp.where(keep, pltpu.roll(e, SUBLANES - dlt, 0), 0.0)
            psh = jnp.where(keep, pltpu.roll(p, SUBLANES - dlt, 0), 1.0)
            e = p * es + e
            p = p * psh
        full = e + p * carry
        cin = jnp.where(rid < SUBLANES - 1, pltpu.roll(full, SUBLANES - 1, 0), carry)
        for j in range(sub):
            hb = hs[j] + ps[j] * cin
            hsum = hb + hf_ref[pl.ds(t0 + j * SUBLANES, SUBLANES), :]
            out_ref[0, 0, pl.ds(t0 + j, SUBLANES, stride=sub), :] = hsum
        return full[0:1]

    zero = jnp.zeros((1, LANES), F32)
    lax.fori_loop(0, n_chunks, fwd_body, zero)
    lax.fori_loop(0, n_chunks, bwd_body, zero)


def _lru(xa, conv_w, conv_b, wg, bg, lam):
    b, nh, s, _ = xa.shape
    tc = SUBLANES * LRU_SUB
    n_chunks = -(-s // tc)
    sp = n_chunks * tc
    kern = functools.partial(_lru_kernel, seq=s, sub=LRU_SUB, n_chunks=n_chunks)
    return pl.pallas_call(
        kern,
        grid=(b, nh),
        in_specs=[pl.BlockSpec((1, 1, s, LANES), lambda i, h: (i, h, 0, 0)),
                  pl.BlockSpec((4, LANES), lambda i, h: (0, h)),
                  pl.BlockSpec((1, LANES), lambda i, h: (0, h)),
                  pl.BlockSpec((2, 1, LANES, 2 * LANES), lambda i, h: (0, h, 0, 0)),
                  pl.BlockSpec((2, 1, 1, 2 * LANES), lambda i, h: (0, h, 0, 0)),
                  pl.BlockSpec((2, 1, 1, LANES), lambda i, h: (0, h, 0, 0))],
        out_specs=pl.BlockSpec((1, 1, sp, LANES), lambda i, h: (i, h, 0, 0)),
        out_shape=jax.ShapeDtypeStruct((b, nh, sp, LANES), F32),
        scratch_shapes=[pltpu.VMEM((sp + 2 * SUBLANES, LANES), F32),
                        pltpu.VMEM((sp, LANES), F32)],
        compiler_params=pltpu.CompilerParams(
            dimension_semantics=("parallel", "parallel"), vmem_limit_bytes=VMEM_LIMIT),
        name="lru",
    )(xa, conv_w, conv_b, wg, bg, lam)


def _seqdft_kernel(ct_ref, st_ref, pq_ref, o_ref, *, scale):
    half = N_GROUPS * LANES
    f = jnp.dot(ct_ref[...], pq_ref[0, :, :half], preferred_element_type=F32)
    f = f - jnp.dot(st_ref[...], pq_ref[0, :, half:], preferred_element_type=F32)
    o_ref[0] = (f * scale).astype(BF16)


def _seqdft(pq, cos_t, sin_t, tk):
    b, s, w = pq.shape
    half = w // 2
    kern = functools.partial(_seqdft_kernel, scale=1.0 / math.sqrt(s * LANES))
    return pl.pallas_call(
        kern,
        grid=(b, s // tk),
        in_specs=[pl.BlockSpec((tk, s), lambda i, k: (k, 0)),
                  pl.BlockSpec((tk, s), lambda i, k: (k, 0)),
                  pl.BlockSpec((1, s, w), lambda i, k: (i, 0, 0), pipeline_mode=pl.Buffered(1))],
        out_specs=pl.BlockSpec((1, tk, half), lambda i, k: (i, k, 0)),
        out_shape=jax.ShapeDtypeStruct((b, s, half), BF16),
        compiler_params=pltpu.CompilerParams(
            dimension_semantics=("parallel", "arbitrary"), vmem_limit_bytes=VMEM_LIMIT),
        name="seqdft",
    )(cos_t, sin_t, pq)


def _dft_tables(s, tk):
    nk = s // tk
    t = jnp.arange(s, dtype=jnp.int32)
    k0 = jnp.arange(nk, dtype=jnp.int32) * tk
    ki = jnp.arange(tk, dtype=jnp.int32)
    w = 2.0 * math.pi / s
    ang_a = ((k0[:, None] * t[None, :]) % s).astype(F32) * w
    ang_b = ((ki[:, None] * t[None, :]) % s).astype(F32) * w
    ca, sa = jnp.cos(ang_a)[:, None, :], jnp.sin(ang_a)[:, None, :]
    cb, sb = jnp.cos(ang_b)[None], jnp.sin(ang_b)[None]
    cos_t = (ca * cb - sa * sb).reshape(s, s).astype(BF16)
    sin_t = (sa * cb + ca * sb).reshape(s, s).astype(BF16)
    return cos_t, sin_t


def _merge_kernel(hs_ref, yag_ref, f_ref, x_ref, mod_ref, gpost_ref, gpre_ref,
                  wa_ref, wb_ref, wo_ref, x1_ref, h2_ref):
    d = x_ref.shape[-1]
    m = mod_ref[0]
    hsum = jnp.concatenate([hs_ref[0, hh] for hh in range(N_HEADS)], axis=1)
    ya = yag_ref[0, :, 0:d].astype(F32)
    z = (hsum * _gelu(ya)).astype(BF16)
    y_a = jnp.dot(z, wa_ref[...], preferred_element_type=F32)
    y_b = jnp.dot(f_ref[0], wb_ref[...], preferred_element_type=F32)
    ga = yag_ref[0, :, d:2 * d].astype(F32)
    gb = yag_ref[0, :, 2 * d:3 * d].astype(F32)
    merged = _sigmoid(ga) * y_a + _sigmoid(gb) * y_b
    mo = jnp.dot(merged.astype(BF16), wo_ref[...], preferred_element_type=F32)
    x1 = x_ref[0] + m[2:3] * _rms(mo, gpost_ref[...])
    x1_ref[0] = x1
    h2 = _rms(x1, gpre_ref[...]) * (1.0 + m[4:5]) + m[3:4]
    h2_ref[0] = h2.astype(BF16)


def _merge(hs, yag, fo, x, mod, g_post, g_pre2, wa, wb, wo, ts):
    b, s, d = x.shape
    tok = lambda i, j: (i, j, 0)
    return pl.pallas_call(
        _merge_kernel,
        grid=(b, s // ts),
        in_specs=[pl.BlockSpec((1, N_HEADS, ts, LANES), lambda i, j: (i, 0, j, 0)),
                  pl.BlockSpec((1, ts, 3 * d), tok),
                  pl.BlockSpec((1, ts, fo.shape[-1]), tok),
                  pl.BlockSpec((1, ts, d), tok),
                  pl.BlockSpec((1, N_MOD, d), lambda i, j: (i, 0, 0)),
                  _const_spec((1, d)), _const_spec((1, d)),
                  _const_spec(wa.shape), _const_spec(wb.shape), _const_spec(wo.shape)],
        out_specs=[pl.BlockSpec((1, ts, d), tok), pl.BlockSpec((1, ts, d), tok)],
        out_shape=[jax.ShapeDtypeStruct((b, s, d), F32), jax.ShapeDtypeStruct((b, s, d), BF16)],
        compiler_params=pltpu.CompilerParams(
            dimension_semantics=("parallel", "parallel"), vmem_limit_bytes=VMEM_LIMIT),
        name="merge",
    )(hs, yag, fo, x, mod, g_post, g_pre2, wa, wb, wo)


def _ffn_kernel(h2_ref, prev_ref, next_ref, x1_ref, mod_ref, gpost_ref,
                wup_ref, cw_ref, cb_ref, wdn_ref, o_ref):
    ts = h2_ref.shape[1]
    i = pl.program_id(1)
    last = pl.num_programs(1) - 1
    halo = BF16_ROWS
    top = jnp.where(i > 0, prev_ref[0], jnp.zeros_like(prev_ref[0]))
    bot = jnp.where(i < last, next_ref[0], jnp.zeros_like(next_ref[0]))
    hcat = jnp.concatenate([top, h2_ref[0], bot], axis=0)
    rows = ts + 2 * halo
    n_chunks = wup_ref.shape[0]
    acc = jnp.zeros((ts, o_ref.shape[-1]), F32)
    for c in range(n_chunks):
        up = jnp.dot(hcat, wup_ref[c], preferred_element_type=F32)
        w = cw_ref[c]
        conv = (cb_ref[c] + w[1:2] * up[halo:halo + ts]
                + w[0:1] * pltpu.roll(up, 1, 0)[halo:halo + ts]
                + w[2:3] * pltpu.roll(up, rows - 1, 0)[halo:halo + ts])
        act = (_gelu(conv[:, FFN_CHUNK:]) * conv[:, :FFN_CHUNK]).astype(BF16)
        acc = acc + jnp.dot(act, wdn_ref[c], preferred_element_type=F32)
    m = mod_ref[0]
    o_ref[0] = x1_ref[0] + m[5:6] * _rms(acc, gpost_ref[...])


def _ffn(h2, x1, mod, g_post, wup_r, cw_r, cb_r, wdn_r, ts):
    b, s, d = x1.shape
    tok = lambda i, j: (i, j, 0)
    hb = ts // BF16_ROWS
    nblk = s // BF16_ROWS
    return pl.pallas_call(
        _ffn_kernel,
        grid=(b, s // ts),
        in_specs=[pl.BlockSpec((1, ts, d), tok),
                  pl.BlockSpec((1, BF16_ROWS, d), lambda i, j: (i, jnp.maximum(j * hb - 1, 0), 0)),
                  pl.BlockSpec((1, BF16_ROWS, d), lambda i, j: (i, jnp.minimum((j + 1) * hb, nblk - 1), 0)),
                  pl.BlockSpec((1, ts, d), tok),
                  pl.BlockSpec((1, N_MOD, d), lambda i, j: (i, 0, 0)),
                  _const_spec((1, d)),
                  _const_spec(wup_r.shape), _const_spec(cw_r.shape), _const_spec(cb_r.shape),
                  _const_spec(wdn_r.shape)],
        out_specs=pl.BlockSpec((1, ts, d), tok),
        out_shape=jax.ShapeDtypeStruct((b, s, d), F32),
        compiler_params=pltpu.CompilerParams(
            dimension_semantics=("parallel", "arbitrary"), vmem_limit_bytes=VMEM_LIMIT),
        name="ffn",
    )(h2, h2, h2, x1, mod, g_post, wup_r, cw_r, cb_r, wdn_r)


def _prepare_params(g_pre_mix, w_in, conv_w, conv_b, w_lru_a, b_lru_a, w_lru_x, b_lru_x, lru_lambda,
                    w_a_out, w_b_out, w_o, g_post_mix, g_pre_ffn, w_up, ffn_conv_w, ffn_conv_b,
                    w_down, g_post_ffn):
    d = w_in.shape[0]
    d_ff = w_down.shape[0]
    d_f = w_b_out.shape[0]
    w_in_r = jnp.concatenate([w_in[:, :2 * d], w_in[:, 2 * d + d_f:], w_in[:, 2 * d:2 * d + d_f]],
                             axis=1).astype(BF16)
    ang = (2.0 * math.pi / LANES) * ((jnp.arange(LANES, dtype=jnp.int32)[:, None]
                                      * jnp.arange(LANES, dtype=jnp.int32)[None, :]) % LANES).astype(F32)
    dft_g = jnp.concatenate([jnp.cos(ang), jnp.sin(ang)], axis=1).astype(BF16)
    wg = jnp.concatenate([w_lru_a, w_lru_x], axis=-1).astype(BF16)
    bg = jnp.concatenate([b_lru_a.reshape(2, N_HEADS, 1, LANES),
                          b_lru_x.reshape(2, N_HEADS, 1, LANES)], axis=-1)
    lam = lru_lambda.reshape(2, N_HEADS, 1, LANES)
    nc = d_ff // FFN_CHUNK
    val_w = w_up[:, :d_ff].reshape(d, nc, FFN_CHUNK)
    gate_w = w_up[:, d_ff:].reshape(d, nc, FFN_CHUNK)
    wup_r = jnp.concatenate([val_w, gate_w], axis=-1).transpose(1, 0, 2).astype(BF16)
    cw_r = jnp.concatenate([ffn_conv_w[:, :d_ff].reshape(3, nc, FFN_CHUNK),
                            ffn_conv_w[:, d_ff:].reshape(3, nc, FFN_CHUNK)], axis=-1).transpose(1, 0, 2)
    cb_r = jnp.concatenate([ffn_conv_b[:d_ff].reshape(nc, 1, FFN_CHUNK),
                            ffn_conv_b[d_ff:].reshape(nc, 1, FFN_CHUNK)], axis=-1)
    wdn_r = w_down.reshape(nc, FFN_CHUNK, d).astype(BF16)
    return dict(g_pre_mix=g_pre_mix.reshape(1, d), w_in_r=w_in_r, dft_g=dft_g,
                conv_w=conv_w, conv_b=conv_b.reshape(1, d), wg=wg, bg=bg, lam=lam,
                wa=w_a_out.astype(BF16), wb=w_b_out.astype(BF16), wo=w_o.astype(BF16),
                g_post_mix=g_post_mix.reshape(1, d), g_pre_ffn=g_pre_ffn.reshape(1, d),
                wup_r=wup_r, cw_r=cw_r, cb_r=cb_r, wdn_r=wdn_r, g_post_ffn=g_post_ffn.reshape(1, d))


def _encoder_layer(x, mod, p, ts, tk):
    s = x.shape[1]
    xa, yag, pq = _inproj(x, mod, p["g_pre_mix"], p["w_in_r"], p["dft_g"], ts)
    hs = _lru(xa, p["conv_w"], p["conv_b"], p["wg"], p["bg"], p["lam"])
    cos_t, sin_t = _dft_tables(s, tk)
    fo = _seqdft(pq, cos_t, sin_t, tk)
    x1, h2 = _merge(hs, yag, fo, x, mod, p["g_post_mix"], p["g_pre_ffn"], p["wa"], p["wb"], p["wo"], ts)
    return _ffn(h2, x1, mod, p["g_post_ffn"], p["wup_r"], p["cw_r"], p["cb_r"], p["wdn_r"], ts)


def _seqdft_tile(s):
    return max(SUBLANES * 16, min(512, (2 * 1024 * 1024) // s))


def kernel(x_prompt, x_sample, c_prompt, c_sample, w_ada, b_ada, g_pre_mix, w_in, conv_w, conv_b, w_lru_a, b_lru_a, w_lru_x, b_lru_x, lru_lambda, w_a_out, w_b_out, w_o, g_post_mix, g_pre_ffn, w_up, ffn_conv_w, ffn_conv_b, w_down, g_post_ffn):
    depth = w_ada.shape[0]
    nb_p = x_prompt.shape[0]
    d = x_prompt.shape[-1]
    y_p, y_s = x_prompt, x_sample
    c_all = jnp.concatenate([c_prompt, c_sample], axis=0)
    for l in range(depth):
        p = _prepare_params(g_pre_mix[l], w_in[l], conv_w[l], conv_b[l], w_lru_a[l], b_lru_a[l],
                            w_lru_x[l], b_lru_x[l], lru_lambda[l], w_a_out[l], w_b_out[l], w_o[l],
                            g_post_mix[l], g_pre_ffn[l], w_up[l], ffn_conv_w[l], ffn_conv_b[l],
                            w_down[l], g_post_ffn[l])
        mod = _modulation(c_all, w_ada[l], b_ada[l]).reshape(-1, N_MOD, d)
        y_p = _encoder_layer(y_p, mod[:nb_p], p, 512, _seqdft_tile(y_p.shape[1]))
        y_s = _encoder_layer(y_s, mod[nb_p:], p, 512, _seqdft_tile(y_s.shape[1]))
    return (y_p, y_s)
```

```python
import functools
import math

import jax
import jax.numpy as jnp
from jax import lax
from jax.experimental import pallas as pl
from jax.experimental.pallas import tpu as pltpu

F32 = jnp.float32
BF16 = jnp.bfloat16

LANES = 128
SUBLANES = 8
BF16_ROWS = 16

N_HEADS = 8
N_GROUPS = 4
N_MOD = 6
LRU_C = 8.0
RMS_EPS = 1e-6
GELU_K = math.sqrt(2.0 / math.pi)
LOG2_E = 1.0 / math.log(2.0)

LRU_SUBS = (52, 44, 60, 36, 68, 28, 76, 20, 12)
LRU_VMEM_BYTES_PER_ELEM = 2 * 2 + 2 * 4 + 3 * 4
FFN_CHUNK = 256
VMEM_LIMIT = 56 * 1024 * 1024


def _gelu(x):
    return 0.5 * x * (1.0 + jnp.tanh(GELU_K * (x + 0.044715 * (x * x * x))))


def _sigmoid(x):
    return 0.5 * (1.0 + jnp.tanh(0.5 * x))


def _rms(x, g):
    return x * lax.rsqrt(jnp.mean(x * x, axis=-1, keepdims=True) + RMS_EPS) * g


def _const_spec(shape):
    nd = len(shape)
    return pl.BlockSpec(shape, lambda *_: (0,) * nd, pipeline_mode=pl.Buffered(1))


def _mod_kernel(c_ref, w_ref, b_ref, o_ref):
    c = c_ref[...]
    s = c * _sigmoid(c)
    o_ref[...] = jnp.dot(s, w_ref[...], preferred_element_type=F32) + b_ref[...]


def _modulation(c, w_ada, b_ada):
    nb, d = c.shape
    n = w_ada.shape[1]
    tn = 512
    return pl.pallas_call(
        _mod_kernel,
        grid=(n // tn,),
        in_specs=[pl.BlockSpec((nb, d), lambda j: (0, 0)),
                  pl.BlockSpec((d, tn), lambda j: (0, j)),
                  pl.BlockSpec((1, tn), lambda j: (0, j))],
        out_specs=pl.BlockSpec((nb, tn), lambda j: (0, j)),
        out_shape=jax.ShapeDtypeStruct((nb, n), F32),
        name="mod",
    )(c, w_ada, b_ada.reshape(1, n))


def _inproj_kernel(x_ref, mod_ref, g_ref, w_ref, dft_ref, xa_ref, yag_ref, pq_ref):
    d = x_ref.shape[-1]
    x = x_ref[0]
    m = mod_ref[0]
    h = _rms(x, g_ref[...]) * (1.0 + m[1:2]) + m[0:1]
    hb = h.astype(BF16)
    xa = jnp.dot(hb, w_ref[:, 0:d], preferred_element_type=F32)
    for hh in range(N_HEADS):
        xa_ref[0, hh] = xa[:, hh * LANES:(hh + 1) * LANES].astype(BF16)
    yag_ref[0] = jnp.dot(hb, w_ref[:, d:4 * d], preferred_element_type=F32).astype(BF16)
    xb = jnp.dot(hb, w_ref[:, 4 * d:], preferred_element_type=F32).astype(BF16)
    half = N_GROUPS * LANES
    for g in range(N_GROUPS):
        pq = jnp.dot(xb[:, g * LANES:(g + 1) * LANES], dft_ref[...], preferred_element_type=F32)
        pq_ref[0, :, g * LANES:(g + 1) * LANES] = pq[:, :LANES].astype(BF16)
        pq_ref[0, :, half + g * LANES:half + (g + 1) * LANES] = pq[:, LANES:].astype(BF16)


def _inproj(x, mod, g_pre, w_in_r, dft_g, ts):
    b, s, d = x.shape
    n = w_in_r.shape[1]
    return pl.pallas_call(
        _inproj_kernel,
        grid=(b, s // ts),
        in_specs=[pl.BlockSpec((1, ts, d), lambda i, j: (i, j, 0)),
                  pl.BlockSpec((1, N_MOD, d), lambda i, j: (i, 0, 0)),
                  _const_spec((1, d)),
                  _const_spec((d, n)),
                  _const_spec(dft_g.shape)],
        out_specs=[pl.BlockSpec((1, N_HEADS, ts, LANES), lambda i, j: (i, 0, j, 0)),
                   pl.BlockSpec((1, ts, 3 * d), lambda i, j: (i, j, 0)),
                   pl.BlockSpec((1, ts, 2 * N_GROUPS * LANES), lambda i, j: (i, j, 0))],
        out_shape=[jax.ShapeDtypeStruct((b, N_HEADS, s, LANES), BF16),
                   jax.ShapeDtypeStruct((b, s, 3 * d), BF16),
                   jax.ShapeDtypeStruct((b, s, 2 * N_GROUPS * LANES), BF16)],
        compiler_params=pltpu.CompilerParams(
            dimension_semantics=("parallel", "parallel"), vmem_limit_bytes=VMEM_LIMIT),
        name="inproj",
    )(x, mod, g_pre, w_in_r, dft_g)


def _lru_plan(seq):
    rows = seq // SUBLANES
    assert rows * SUBLANES == seq
    for sub in LRU_SUBS:
        n_main, rem = divmod(rows, sub)
        if rem == 0 or rem % 8 == 4:
            return sub, n_main, rem
    raise ValueError(f"no RG-LRU chunk plan for sequence length {seq}")


def _lru_kernel(xa_ref, cw_ref, cb_ref, wg_ref, bg_ref, lam_ref, out_ref, xp_ref, xc_ref, hf_ref,
                *, seq, sub, n_main, sub_tail):
    pad = SUBLANES
    heads = range(xa_ref.shape[1])
    for hd in heads:
        xp_ref[hd, 0:pad, :] = jnp.zeros((pad, LANES), F32)
        xp_ref[hd, pad:pad + seq, :] = xa_ref[0, hd].astype(F32)
        xp_ref[hd, pad + seq:, :] = jnp.zeros((pad, LANES), F32)

    rid = lax.broadcasted_iota(jnp.int32, (SUBLANES, LANES), 0)

    def decay_scale(direction, hd):
        lam = lam_ref[direction, hd]
        softplus_neg = jnp.maximum(-lam, 0.0) + jnp.log1p(jnp.exp(-jnp.abs(lam)))
        return (-0.5 * LRU_C * LOG2_E) * softplus_neg

    def conv_half(hd, t0, sb):
        lanes = slice(hd * LANES, (hd + 1) * LANES)
        cw = cw_ref[:, lanes]
        cb = cb_ref[:, lanes]
        base = pad + t0 - 2
        taps = [xp_ref[hd, pl.ds(base + m, SUBLANES, stride=sb), :] for m in range(sb + 3)]
        pieces = []
        for j in range(sb):
            acc = cb + cw[0:1] * taps[j]
            for k in range(1, 4):
                acc = acc + cw[k:k + 1] * taps[j + k]
            pieces.append(acc)
        return jnp.concatenate(pieces, axis=0)

    def gates(xh, direction, hd, c1):
        pre = jnp.dot(xh.astype(BF16), wg_ref[direction, hd], preferred_element_type=F32)
        pre = pre + bg_ref[direction, hd]
        tr = jnp.tanh(pre[:, :LANES])
        ti = jnp.tanh(pre[:, LANES:])
        a = jnp.exp2(c1 + c1 * tr)
        y = 1.0 - a * a
        mult = jnp.where(y > 0.0, y * lax.rsqrt(y), 0.0)
        g = mult * xh
        return a, g + g * ti

    def local_scan(a, b, order, sb):
        h = jnp.zeros((SUBLANES, LANES), F32)
        p = jnp.ones((SUBLANES, LANES), F32)
        hs = [None] * sb
        ps = [None] * sb
        for j in order:
            aj = a[j * SUBLANES:(j + 1) * SUBLANES]
            h = aj * h + b[j * SUBLANES:(j + 1) * SUBLANES]
            p = aj * p
            hs[j] = h
            ps[j] = p
        return hs, ps, h, p

    def fwd_chunk(hd, t0, sb, carry, c1):
        xh = conv_half(hd, t0, sb)
        xc_ref[hd, pl.ds(t0, SUBLANES * sb), :] = xh
        a, b = gates(xh, 0, hd, c1)
        hs, ps, e, p = local_scan(a, b, range(sb), sb)
        for dlt in (1, 2, 4):
            keep = rid >= dlt
            es = jnp.where(keep, pltpu.roll(e, dlt, 0), 0.0)
            psh = jnp.where(keep, pltpu.roll(p, dlt, 0), 1.0)
            e = p * es + e
            p = p * psh
        full = e + p * carry
        cin = jnp.where(rid >= 1, pltpu.roll(full, 1, 0), carry)
        for j in range(sb):
            hf_ref[hd, pl.ds(t0 + j * SUBLANES, SUBLANES), :] = hs[j] + ps[j] * cin
        return full[SUBLANES - 1:SUBLANES]

    def bwd_chunk(hd, t0, sb, carry, c1):
        xh = xc_ref[hd, pl.ds(t0, SUBLANES * sb), :]
        a, b = gates(xh, 1, hd, c1)
        hs, ps, e, p = local_scan(a, b, range(sb - 1, -1, -1), sb)
        for dlt in (1, 2, 4):
            keep = rid < SUBLANES - dlt
            es = jnp.where(keep, pltpu.roll(e, SUBLANES - dlt, 0), 0.0)
            psh = jnp.where(keep, pltpu.roll(p, SUBLANES - dlt, 0), 1.0)
            e = p * es + e
            p = p * psh
        full = e + p * carry
        cin = jnp.where(rid < SUBLANES - 1, pltpu.roll(full, SUBLANES - 1, 0), carry)
        for j in range(sb):
            hsum = hs[j] + ps[j] * cin + hf_ref[hd, pl.ds(t0 + j * SUBLANES, SUBLANES), :]
            out_ref[0, hd, pl.ds(t0 + j, SUBLANES, stride=sb), :] = hsum
        return full[0:1]

    tc = SUBLANES * sub
    t_tail = n_main * tc
    zeros = tuple(jnp.zeros((1, LANES), F32) for _ in heads)

    def all_heads(step, t0, sb, carries, scales):
        return tuple(step(hd, t0, sb, carries[hd], scales[hd]) for hd in heads)

    scales = tuple(decay_scale(0, hd) for hd in heads)
    carries = lax.fori_loop(
        0, n_main,
        lambda c, cr: all_heads(fwd_chunk, pl.multiple_of(c * tc, SUBLANES), sub, cr, scales), zeros)
    if sub_tail:
        all_heads(fwd_chunk, t_tail, sub_tail, carries, scales)

    scales = tuple(decay_scale(1, hd) for hd in heads)
    carries = all_heads(bwd_chunk, t_tail, sub_tail, zeros, scales) if sub_tail else zeros
    lax.fori_loop(
        0, n_main,
        lambda cc, cr: all_heads(bwd_chunk, pl.multiple_of((n_main - 1 - cc) * tc, SUBLANES), sub, cr, scales),
        carries)


def _lru(xa, conv_w_half, conv_b_half, wg, bg_half, lam):
    b, nh, s, _ = xa.shape
    hp = max(h for h in (4, 2, 1) if h * s * LANES * LRU_VMEM_BYTES_PER_ELEM <= VMEM_LIMIT - (4 << 20))
    sub, n_main, sub_tail = _lru_plan(s)
    kern = functools.partial(_lru_kernel, seq=s, sub=sub, n_main=n_main, sub_tail=sub_tail)
    return pl.pallas_call(
        kern,
        grid=(b, nh // hp),
        in_specs=[pl.BlockSpec((1, hp, s, LANES), lambda i, h: (i, h, 0, 0)),
                  pl.BlockSpec((4, hp * LANES), lambda i, h: (0, h)),
                  pl.BlockSpec((1, hp * LANES), lambda i, h: (0, h)),
                  pl.BlockSpec((2, hp, LANES, 2 * LANES), lambda i, h: (0, h, 0, 0)),
                  pl.BlockSpec((2, hp, 1, 2 * LANES), lambda i, h: (0, h, 0, 0)),
                  pl.BlockSpec((2, hp, 1, LANES), lambda i, h: (0, h, 0, 0))],
        out_specs=pl.BlockSpec((1, hp, s, LANES), lambda i, h: (i, h, 0, 0)),
        out_shape=jax.ShapeDtypeStruct((b, nh, s, LANES), F32),
        scratch_shapes=[pltpu.VMEM((hp, s + 2 * SUBLANES, LANES), F32),
                        pltpu.VMEM((hp, s, LANES), F32),
                        pltpu.VMEM((hp, s, LANES), F32)],
        compiler_params=pltpu.CompilerParams(
            dimension_semantics=("parallel", "parallel"), vmem_limit_bytes=VMEM_LIMIT),
        name="lru",
    )(xa, conv_w_half, conv_b_half, wg, bg_half, lam)


def _seqdft_kernel(ct_ref, st_ref, pq_ref, o_ref, *, scale):
    half = N_GROUPS * LANES
    f = jnp.dot(ct_ref[...], pq_ref[0, :, :half], preferred_element_type=F32)
    f = f - jnp.dot(st_ref[...], pq_ref[0, :, half:], preferred_element_type=F32)
    o_ref[0] = (f * scale).astype(BF16)


def _seqdft(pq, cos_t, sin_t, tk):
    b, s, w = pq.shape
    half = w // 2
    kern = functools.partial(_seqdft_kernel, scale=1.0 / math.sqrt(s * LANES))
    return pl.pallas_call(
        kern,
        grid=(b, s // tk),
        in_specs=[pl.BlockSpec((tk, s), lambda i, k: (k, 0)),
                  pl.BlockSpec((tk, s), lambda i, k: (k, 0)),
                  pl.BlockSpec((1, s, w), lambda i, k: (i, 0, 0), pipeline_mode=pl.Buffered(1))],
        out_specs=pl.BlockSpec((1, tk, half), lambda i, k: (i, k, 0)),
        out_shape=jax.ShapeDtypeStruct((b, s, half), BF16),
        compiler_params=pltpu.CompilerParams(
            dimension_semantics=("parallel", "arbitrary"), vmem_limit_bytes=VMEM_LIMIT),
        name="seqdft",
    )(cos_t, sin_t, pq)


def _dft_tables(s, tk):
    nk = s // tk
    t = jnp.arange(s, dtype=jnp.int32)
    k0 = jnp.arange(nk, dtype=jnp.int32) * tk
    ki = jnp.arange(tk, dtype=jnp.int32)
    w = 2.0 * math.pi / s
    ang_a = ((k0[:, None] * t[None, :]) % s).astype(F32) * w
    ang_b = ((ki[:, None] * t[None, :]) % s).astype(F32) * w
    ca, sa = jnp.cos(ang_a)[:, None, :], jnp.sin(ang_a)[:, None, :]
    cb, sb = jnp.cos(ang_b)[None], jnp.sin(ang_b)[None]
    cos_t = (ca * cb - sa * sb).reshape(s, s).astype(BF16)
    sin_t = (sa * cb + ca * sb).reshape(s, s).astype(BF16)
    return cos_t, sin_t


def _merge_kernel(hs_ref, yag_ref, f_ref, x_ref, mod_ref, gpost_ref, gpre_ref,
                  wa_ref, wb_ref, wo_ref, x1_ref, h2_ref):
    d = x_ref.shape[-1]
    m = mod_ref[0]
    hsum = jnp.concatenate([hs_ref[0, hh] for hh in range(N_HEADS)], axis=1)
    ya = yag_ref[0, :, 0:d].astype(F32)
    z = (hsum * _gelu(ya)).astype(BF16)
    y_a = jnp.dot(z, wa_ref[...], preferred_element_type=F32)
    y_b = jnp.dot(f_ref[0], wb_ref[...], preferred_element_type=F32)
    ga = yag_ref[0, :, d:2 * d].astype(F32)
    gb = yag_ref[0, :, 2 * d:3 * d].astype(F32)
    merged = _sigmoid(ga) * y_a + _sigmoid(gb) * y_b
    mo = jnp.dot(merged.astype(BF16), wo_ref[...], preferred_element_type=F32)
    x1 = x_ref[0] + m[2:3] * _rms(mo, gpost_ref[...])
    x1_ref[0] = x1
    h2 = _rms(x1, gpre_ref[...]) * (1.0 + m[4:5]) + m[3:4]
    h2_ref[0] = h2.astype(BF16)


def _merge(hs, yag, fo, x, mod, g_post, g_pre2, wa, wb, wo, ts):
    b, s, d = x.shape
    tok = lambda i, j: (i, j, 0)
    return pl.pallas_call(
        _merge_kernel,
        grid=(b, s // ts),
        in_specs=[pl.BlockSpec((1, N_HEADS, ts, LANES), lambda i, j: (i, 0, j, 0)),
                  pl.BlockSpec((1, ts, 3 * d), tok),
                  pl.BlockSpec((1, ts, fo.shape[-1]), tok),
                  pl.BlockSpec((1, ts, d), tok),
                  pl.BlockSpec((1, N_MOD, d), lambda i, j: (i, 0, 0)),
                  _const_spec((1, d)), _const_spec((1, d)),
                  _const_spec(wa.shape), _const_spec(wb.shape), _const_spec(wo.shape)],
        out_specs=[pl.BlockSpec((1, ts, d), tok), pl.BlockSpec((1, ts, d), tok)],
        out_shape=[jax.ShapeDtypeStruct((b, s, d), F32), jax.ShapeDtypeStruct((b, s, d), BF16)],
        compiler_params=pltpu.CompilerParams(
            dimension_semantics=("parallel", "parallel"), vmem_limit_bytes=VMEM_LIMIT),
        name="merge",
    )(hs, yag, fo, x, mod, g_post, g_pre2, wa, wb, wo)


def _ffn_kernel(h2_ref, prev_ref, next_ref, x1_ref, mod_ref, gpost_ref,
                wup_ref, cw_ref, cb_ref, wdn_ref, o_ref, hcat_ref, up0_ref, up1_ref, acc_ref):
    ts = h2_ref.shape[1]
    i = pl.program_id(1)
    last = pl.num_programs(1) - 1
    halo = BF16_ROWS
    hcat_ref[0:halo, :] = jnp.where(i > 0, prev_ref[0], jnp.zeros_like(prev_ref[0]))
    hcat_ref[halo:halo + ts, :] = h2_ref[0]
    hcat_ref[halo + ts:, :] = jnp.where(i < last, next_ref[0], jnp.zeros_like(next_ref[0]))
    n_chunks = wup_ref.shape[0]
    up_bufs = (up0_ref, up1_ref)
    n_slabs = up0_ref.shape[0]
    half = n_slabs // 2

    def up_proj(c, buf):
        up = jnp.dot(hcat_ref[...], wup_ref[c], preferred_element_type=F32)
        for k in range(n_slabs):
            up_bufs[buf][k] = up[:, k * LANES:(k + 1) * LANES]

    def conv_down(c, buf, first=False):
        w = cw_ref[c]
        cb = cb_ref[c]
        conv = []
        for k in range(n_slabs):
            lanes = slice(k * LANES, (k + 1) * LANES)
            before = up_bufs[buf][k, pl.ds(halo - 1, ts, stride=1), :]
            here = up_bufs[buf][k, pl.ds(halo, ts), :]
            after = up_bufs[buf][k, pl.ds(halo + 1, ts, stride=1), :]
            conv.append(cb[:, lanes] + w[0:1, lanes] * before + w[1:2, lanes] * here
                        + w[2:3, lanes] * after)
        act = jnp.concatenate([_gelu(conv[half + k]) * conv[k] for k in range(half)], axis=1)
        part = jnp.dot(act.astype(BF16), wdn_ref[c], preferred_element_type=F32)
        if first:
            acc_ref[...] = part
        else:
            acc_ref[...] += part

    up_proj(0, 0)
    up_proj(1, 1)
    conv_down(0, 0, first=True)

    def pair(p, carry):
        c = 2 * p + 1
        up_proj(c + 1, 0)
        conv_down(c, 1)
        up_proj(c + 2, 1)
        conv_down(c + 1, 0)
        return carry

    n_pairs = (n_chunks - 2) // 2
    lax.fori_loop(0, n_pairs, pair, 0)
    for c in range(2 * n_pairs + 1, n_chunks):
        if c + 1 < n_chunks:
            up_proj(c + 1, (c + 1) % 2)
        conv_down(c, c % 2)
    m = mod_ref[0]
    o_ref[0] = x1_ref[0] + m[5:6] * _rms(acc_ref[...], gpost_ref[...])


def _ffn(h2, x1, mod, g_post, wup_r, cw_r, cb_r, wdn_r, ts):
    b, s, d = x1.shape
    tok = lambda i, j: (i, j, 0)
    hb = ts // BF16_ROWS
    nblk = s // BF16_ROWS
    return pl.pallas_call(
        _ffn_kernel,
        grid=(b, s // ts),
        in_specs=[pl.BlockSpec((1, ts, d), tok),
                  pl.BlockSpec((1, BF16_ROWS, d), lambda i, j: (i, jnp.maximum(j * hb - 1, 0), 0)),
                  pl.BlockSpec((1, BF16_ROWS, d), lambda i, j: (i, jnp.minimum((j + 1) * hb, nblk - 1), 0)),
                  pl.BlockSpec((1, ts, d), tok),
                  pl.BlockSpec((1, N_MOD, d), lambda i, j: (i, 0, 0)),
                  _const_spec((1, d)),
                  _const_spec(wup_r.shape), _const_spec(cw_r.shape), _const_spec(cb_r.shape),
                  _const_spec(wdn_r.shape)],
        out_specs=pl.BlockSpec((1, ts, d), tok),
        out_shape=jax.ShapeDtypeStruct((b, s, d), F32),
        scratch_shapes=[pltpu.VMEM((ts + 2 * BF16_ROWS, d), BF16),
                        pltpu.VMEM((wup_r.shape[-1] // LANES, ts + 2 * BF16_ROWS, LANES), F32),
                        pltpu.VMEM((wup_r.shape[-1] // LANES, ts + 2 * BF16_ROWS, LANES), F32),
                        pltpu.VMEM((ts, d), F32)],
        compiler_params=pltpu.CompilerParams(
            dimension_semantics=("parallel", "arbitrary"), vmem_limit_bytes=VMEM_LIMIT),
        name="ffn",
    )(h2, h2, h2, x1, mod, g_post, wup_r, cw_r, cb_r, wdn_r)


def _prepare_params(g_pre_mix, w_in, conv_w, conv_b, w_lru_a, b_lru_a, w_lru_x, b_lru_x, lru_lambda,
                    w_a_out, w_b_out, w_o, g_post_mix, g_pre_ffn, w_up, ffn_conv_w, ffn_conv_b,
                    w_down, g_post_ffn):
    d = w_in.shape[0]
    d_ff = w_down.shape[0]
    d_f = w_b_out.shape[0]
    w_in_r = jnp.concatenate([w_in[:, :2 * d], w_in[:, 2 * d + d_f:], w_in[:, 2 * d:2 * d + d_f]],
                             axis=1).astype(BF16)
    ang = (2.0 * math.pi / LANES) * ((jnp.arange(LANES, dtype=jnp.int32)[:, None]
                                      * jnp.arange(LANES, dtype=jnp.int32)[None, :]) % LANES).astype(F32)
    dft_g = jnp.concatenate([jnp.cos(ang), jnp.sin(ang)], axis=1).astype(BF16)
    wg = jnp.concatenate([w_lru_a, w_lru_x], axis=-1).astype(BF16)
    bg = jnp.concatenate([b_lru_a.reshape(2, N_HEADS, 1, LANES),
                          b_lru_x.reshape(2, N_HEADS, 1, LANES)], axis=-1)
    lam = lru_lambda.reshape(2, N_HEADS, 1, LANES)
    nc = d_ff // FFN_CHUNK
    val_w = w_up[:, :d_ff].reshape(d, nc, FFN_CHUNK)
    gate_w = w_up[:, d_ff:].reshape(d, nc, FFN_CHUNK)
    wup_r = jnp.concatenate([val_w, gate_w], axis=-1).transpose(1, 0, 2).astype(BF16)
    cw_r = jnp.concatenate([ffn_conv_w[:, :d_ff].reshape(3, nc, FFN_CHUNK),
                            ffn_conv_w[:, d_ff:].reshape(3, nc, FFN_CHUNK)], axis=-1).transpose(1, 0, 2)
    cb_r = jnp.concatenate([ffn_conv_b[:d_ff].reshape(nc, 1, FFN_CHUNK),
                            ffn_conv_b[d_ff:].reshape(nc, 1, FFN_CHUNK)], axis=-1)
    wdn_r = w_down.reshape(nc, FFN_CHUNK, d).astype(BF16)
    return dict(g_pre_mix=g_pre_mix.reshape(1, d), w_in_r=w_in_r, dft_g=dft_g,
                conv_w_half=0.5 * conv_w, conv_b_half=0.5 * conv_b.reshape(1, d), wg=wg, bg_half=0.5 * bg,
                lam=lam,
                wa=w_a_out.astype(BF16), wb=w_b_out.astype(BF16), wo=w_o.astype(BF16),
                g_post_mix=g_post_mix.reshape(1, d), g_pre_ffn=g_pre_ffn.reshape(1, d),
                wup_r=wup_r, cw_r=cw_r, cb_r=cb_r, wdn_r=wdn_r, g_post_ffn=g_post_ffn.reshape(1, d))


def _encoder_layer(x, mod, p, ts, tk):
    s = x.shape[1]
    xa, yag, pq = _inproj(x, mod, p["g_pre_mix"], p["w_in_r"], p["dft_g"], ts)
    hs = _lru(xa, p["conv_w_half"], p["conv_b_half"], p["wg"], p["bg_half"], p["lam"])
    cos_t, sin_t = _dft_tables(s, tk)
    fo = _seqdft(pq, cos_t, sin_t, tk)
    x1, h2 = _merge(hs, yag, fo, x, mod, p["g_post_mix"], p["g_pre_ffn"], p["wa"], p["wb"], p["wo"], ts)
    return _ffn(h2, x1, mod, p["g_post_ffn"], p["wup_r"], p["cw_r"], p["cb_r"], p["wdn_r"], ts)


def _seqdft_tile(s):
    return max(SUBLANES * 16, min(512, (2 * 1024 * 1024) // s))


def kernel(x_prompt, x_sample, c_prompt, c_sample, w_ada, b_ada, g_pre_mix, w_in, conv_w, conv_b, w_lru_a, b_lru_a, w_lru_x, b_lru_x, lru_lambda, w_a_out, w_b_out, w_o, g_post_mix, g_pre_ffn, w_up, ffn_conv_w, ffn_conv_b, w_down, g_post_ffn):
    depth = w_ada.shape[0]
    nb_p = x_prompt.shape[0]
    d = x_prompt.shape[-1]
    y_p, y_s = x_prompt, x_sample
    c_all = jnp.concatenate([c_prompt, c_sample], axis=0)
    for l in range(depth):
        p = _prepare_params(g_pre_mix[l], w_in[l], conv_w[l], conv_b[l], w_lru_a[l], b_lru_a[l],
                            w_lru_x[l], b_lru_x[l], lru_lambda[l], w_a_out[l], w_b_out[l], w_o[l],
                            g_post_mix[l], g_pre_ffn[l], w_up[l], ffn_conv_w[l], ffn_conv_b[l],
                            w_down[l], g_post_ffn[l])
        mod = _modulation(c_all, w_ada[l], b_ada[l]).reshape(-1, N_MOD, d)
        y_p = _encoder_layer(y_p, mod[:nb_p], p, 512, _seqdft_tile(y_p.shape[1]))
        y_s = _encoder_layer(y_s, mod[nb_p:], p, 512, _seqdft_tile(y_s.shape[1]))
    return (y_p, y_s)
```

```python
import functools
import math

import jax
import jax.numpy as jnp
from jax import lax
from jax.experimental import pallas as pl
from jax.experimental.pallas import tpu as pltpu

F32 = jnp.float32
BF16 = jnp.bfloat16

LANES = 128
SUBLANES = 8
BF16_ROWS = 16

N_HEADS = 8
N_GROUPS = 4
N_MOD = 6
LRU_C = 8.0
RMS_EPS = 1e-6
GELU_K = math.sqrt(2.0 / math.pi)
LOG2_E = 1.0 / math.log(2.0)

LRU_SUBS = (52, 44, 60, 36, 68, 28, 76, 20, 12)
LRU_VMEM_BYTES_PER_ELEM = 2 * 2 + 2 * 4 + 3 * 4
FFN_CHUNK = 256
VMEM_LIMIT = 56 * 1024 * 1024


def _gelu_tanh(x):
    return jnp.tanh(x * (GELU_K + (GELU_K * 0.044715) * (x * x)))


def _sigmoid(x):
    return 0.5 * (1.0 + jnp.tanh(0.5 * x))


def _rms_scaled(x, scale_row):
    return x * lax.rsqrt(jnp.mean(x * x, axis=-1, keepdims=True) + RMS_EPS) * scale_row


def _const_spec(shape):
    nd = len(shape)
    return pl.BlockSpec(shape, lambda *_: (0,) * nd, pipeline_mode=pl.Buffered(1))


def _mod_kernel(c_ref, w_ref, b_ref, o_ref):
    c = c_ref[...]
    s = c * _sigmoid(c)
    o_ref[...] = jnp.dot(s, w_ref[...], preferred_element_type=F32) + b_ref[...]


def _modulation(c, w_ada, b_ada):
    nb, d = c.shape
    n = w_ada.shape[1]
    tn = 512
    return pl.pallas_call(
        _mod_kernel,
        grid=(n // tn,),
        in_specs=[pl.BlockSpec((nb, d), lambda j: (0, 0)),
                  pl.BlockSpec((d, tn), lambda j: (0, j)),
                  pl.BlockSpec((1, tn), lambda j: (0, j))],
        out_specs=pl.BlockSpec((nb, tn), lambda j: (0, j)),
        out_shape=jax.ShapeDtypeStruct((nb, n), F32),
        name="mod",
    )(c, w_ada, b_ada.reshape(1, n))


def _inproj_kernel(x_ref, mod_ref, g_ref, w_ref, dft_ref, xa_ref, yag_ref, pq_ref):
    d = x_ref.shape[-1]
    x = x_ref[0]
    m = mod_ref[0]
    h = _rms_scaled(x, g_ref[...] * (1.0 + m[1:2])) + m[0:1]
    hb = h.astype(BF16)
    xa = jnp.dot(hb, w_ref[:, 0:d], preferred_element_type=F32)
    for hh in range(N_HEADS):
        xa_ref[0, hh] = xa[:, hh * LANES:(hh + 1) * LANES].astype(BF16)
    yag_ref[0] = jnp.dot(hb, w_ref[:, d:4 * d], preferred_element_type=F32).astype(BF16)
    xb = jnp.dot(hb, w_ref[:, 4 * d:], preferred_element_type=F32).astype(BF16)
    half = N_GROUPS * LANES
    for g in range(N_GROUPS):
        pq = jnp.dot(xb[:, g * LANES:(g + 1) * LANES], dft_ref[...], preferred_element_type=F32)
        pq_ref[0, :, g * LANES:(g + 1) * LANES] = pq[:, :LANES].astype(BF16)
        pq_ref[0, :, half + g * LANES:half + (g + 1) * LANES] = pq[:, LANES:].astype(BF16)


def _inproj(x, mod, g_pre, w_in_r, dft_g, ts):
    b, s, d = x.shape
    n = w_in_r.shape[1]
    return pl.pallas_call(
        _inproj_kernel,
        grid=(b, s // ts),
        in_specs=[pl.BlockSpec((1, ts, d), lambda i, j: (i, j, 0)),
                  pl.BlockSpec((1, N_MOD, d), lambda i, j: (i, 0, 0)),
                  _const_spec((1, d)),
                  _const_spec((d, n)),
                  _const_spec(dft_g.shape)],
        out_specs=[pl.BlockSpec((1, N_HEADS, ts, LANES), lambda i, j: (i, 0, j, 0)),
                   pl.BlockSpec((1, ts, 3 * d), lambda i, j: (i, j, 0)),
                   pl.BlockSpec((1, ts, 2 * N_GROUPS * LANES), lambda i, j: (i, j, 0))],
        out_shape=[jax.ShapeDtypeStruct((b, N_HEADS, s, LANES), BF16),
                   jax.ShapeDtypeStruct((b, s, 3 * d), BF16),
                   jax.ShapeDtypeStruct((b, s, 2 * N_GROUPS * LANES), BF16)],
        compiler_params=pltpu.CompilerParams(
            dimension_semantics=("parallel", "parallel"), vmem_limit_bytes=VMEM_LIMIT),
        name="inproj",
    )(x, mod, g_pre, w_in_r, dft_g)


def _lru_plan(seq):
    rows = seq // SUBLANES
    assert rows * SUBLANES == seq
    for sub in LRU_SUBS:
        n_main, rem = divmod(rows, sub)
        if rem == 0 or rem % 8 == 4:
            return sub, n_main, rem
    raise ValueError(f"no RG-LRU chunk plan for sequence length {seq}")


def _lru_kernel(xa_ref, cw_ref, cb_ref, wg_ref, bg_ref, lam_ref, out_ref, xp_ref, xc_ref, hf_ref,
                *, seq, sub, n_main, sub_tail):
    pad = SUBLANES
    heads = range(xa_ref.shape[1])
    for hd in heads:
        xp_ref[hd, 0:pad, :] = jnp.zeros((pad, LANES), F32)
        xp_ref[hd, pad:pad + seq, :] = xa_ref[0, hd].astype(F32)
        xp_ref[hd, pad + seq:, :] = jnp.zeros((pad, LANES), F32)

    rid = lax.broadcasted_iota(jnp.int32, (SUBLANES, LANES), 0)

    def decay_scale(direction, hd):
        lam = lam_ref[direction, hd]
        softplus_neg = jnp.maximum(-lam, 0.0) + jnp.log1p(jnp.exp(-jnp.abs(lam)))
        return (-0.5 * LRU_C * LOG2_E) * softplus_neg

    def conv_half(hd, t0, sb):
        lanes = slice(hd * LANES, (hd + 1) * LANES)
        cw = cw_ref[:, lanes]
        cb = cb_ref[:, lanes]
        base = pad + t0 - 2
        taps = [xp_ref[hd, pl.ds(base + m, SUBLANES, stride=sb), :] for m in range(sb + 3)]
        pieces = []
        for j in range(sb):
            acc = cb + cw[0:1] * taps[j]
            for k in range(1, 4):
                acc = acc + cw[k:k + 1] * taps[j + k]
            pieces.append(acc)
        return jnp.concatenate(pieces, axis=0)

    def gates(xh, direction, hd, c1):
        pre = jnp.dot(xh.astype(BF16), wg_ref[direction, hd], preferred_element_type=F32)
        pre = pre + bg_ref[direction, hd]
        tr = jnp.tanh(pre[:, :LANES])
        ti = jnp.tanh(pre[:, LANES:])
        a = jnp.exp2(c1 + c1 * tr)
        y = 1.0 - a * a
        mult = jnp.where(y > 0.0, y * lax.rsqrt(y), 0.0)
        g = mult * xh
        return a, g + g * ti

    def local_scan(a, b, order, sb):
        h = jnp.zeros((SUBLANES, LANES), F32)
        p = jnp.ones((SUBLANES, LANES), F32)
        hs = [None] * sb
        ps = [None] * sb
        for j in order:
            aj = a[j * SUBLANES:(j + 1) * SUBLANES]
            h = aj * h + b[j * SUBLANES:(j + 1) * SUBLANES]
            p = aj * p
            hs[j] = h
            ps[j] = p
        return hs, ps, h, p

    def fwd_chunk(hd, t0, sb, carry, c1):
        xh = conv_half(hd, t0, sb)
        xc_ref[hd, pl.ds(t0, SUBLANES * sb), :] = xh
        a, b = gates(xh, 0, hd, c1)
        hs, ps, e, p = local_scan(a, b, range(sb), sb)
        for dlt in (1, 2, 4):
            keep = rid >= dlt
            es = jnp.where(keep, pltpu.roll(e, dlt, 0), 0.0)
            psh = jnp.where(keep, pltpu.roll(p, dlt, 0), 1.0)
            e = p * es + e
            p = p * psh
        full = e + p * carry
        cin = jnp.where(rid >= 1, pltpu.roll(full, 1, 0), carry)
        for j in range(sb):
            hf_ref[hd, pl.ds(t0 + j * SUBLANES, SUBLANES), :] = hs[j] + ps[j] * cin
        return full[SUBLANES - 1:SUBLANES]

    def bwd_chunk(hd, t0, sb, carry, c1):
        xh = xc_ref[hd, pl.ds(t0, SUBLANES * sb), :]
        a, b = gates(xh, 1, hd, c1)
        hs, ps, e, p = local_scan(a, b, range(sb - 1, -1, -1), sb)
        for dlt in (1, 2, 4):
            keep = rid < SUBLANES - dlt
            es = jnp.where(keep, pltpu.roll(e, SUBLANES - dlt, 0), 0.0)
            psh = jnp.where(keep, pltpu.roll(p, SUBLANES - dlt, 0), 1.0)
            e = p * es + e
            p = p * psh
        full = e + p * carry
        cin = jnp.where(rid < SUBLANES - 1, pltpu.roll(full, SUBLANES - 1, 0), carry)
        for j in range(sb):
            hsum = hs[j] + ps[j] * cin + hf_ref[hd, pl.ds(t0 + j * SUBLANES, SUBLANES), :]
            out_ref[0, hd, pl.ds(t0 + j, SUBLANES, stride=sb), :] = hsum
        return full[0:1]

    tc = SUBLANES * sub
    t_tail = n_main * tc
    zeros = tuple(jnp.zeros((1, LANES), F32) for _ in heads)

    def all_heads(step, t0, sb, carries, scales):
        return tuple(step(hd, t0, sb, carries[hd], scales[hd]) for hd in heads)

    scales = tuple(decay_scale(0, hd) for hd in heads)
    carries = lax.fori_loop(
        0, n_main,
        lambda c, cr: all_heads(fwd_chunk, pl.multiple_of(c * tc, SUBLANES), sub, cr, scales), zeros)
    if sub_tail:
        all_heads(fwd_chunk, t_tail, sub_tail, carries, scales)

    scales = tuple(decay_scale(1, hd) for hd in heads)
    carries = all_heads(bwd_chunk, t_tail, sub_tail, zeros, scales) if sub_tail else zeros
    lax.fori_loop(
        0, n_main,
        lambda cc, cr: all_heads(bwd_chunk, pl.multiple_of((n_main - 1 - cc) * tc, SUBLANES), sub, cr, scales),
        carries)


def _lru(xa, conv_w_half, conv_b_half, wg, bg_half, lam):
    b, nh, s, _ = xa.shape
    hp = max(h for h in (4, 2, 1) if h * s * LANES * LRU_VMEM_BYTES_PER_ELEM <= VMEM_LIMIT - (4 << 20))
    sub, n_main, sub_tail = _lru_plan(s)
    kern = functools.partial(_lru_kernel, seq=s, sub=sub, n_main=n_main, sub_tail=sub_tail)
    return pl.pallas_call(
        kern,
        grid=(b, nh // hp),
        in_specs=[pl.BlockSpec((1, hp, s, LANES), lambda i, h: (i, h, 0, 0)),
                  pl.BlockSpec((4, hp * LANES), lambda i, h: (0, h)),
                  pl.BlockSpec((1, hp * LANES), lambda i, h: (0, h)),
                  pl.BlockSpec((2, hp, LANES, 2 * LANES), lambda i, h: (0, h, 0, 0)),
                  pl.BlockSpec((2, hp, 1, 2 * LANES), lambda i, h: (0, h, 0, 0)),
                  pl.BlockSpec((2, hp, 1, LANES), lambda i, h: (0, h, 0, 0))],
        out_specs=pl.BlockSpec((1, hp, s, LANES), lambda i, h: (i, h, 0, 0)),
        out_shape=jax.ShapeDtypeStruct((b, nh, s, LANES), F32),
        scratch_shapes=[pltpu.VMEM((hp, s + 2 * SUBLANES, LANES), F32),
                        pltpu.VMEM((hp, s, LANES), F32),
                        pltpu.VMEM((hp, s, LANES), F32)],
        compiler_params=pltpu.CompilerParams(
            dimension_semantics=("parallel", "parallel"), vmem_limit_bytes=VMEM_LIMIT),
        name="lru",
    )(xa, conv_w_half, conv_b_half, wg, bg_half, lam)


def _seqdft_kernel(ct_ref, st_ref, pq_ref, rev_ref, fd_ref, fm_ref, *, scale):
    half = N_GROUPS * LANES
    tk = fd_ref.shape[1]
    u = jnp.dot(ct_ref[0], pq_ref[0, :, :half], preferred_element_type=F32)
    v = jnp.dot(st_ref[0], pq_ref[0, :, half:], preferred_element_type=F32)
    fd_ref[0] = ((u[:tk] - v[:tk]) * scale).astype(BF16)
    mirror = (u + v) * scale
    rev = jnp.dot(rev_ref[...], mirror[:tk].astype(BF16), preferred_element_type=F32)
    row = lax.broadcasted_iota(jnp.int32, (tk, half), 0)
    fm_ref[0] = jnp.where(row == 0, mirror[tk:tk + 1], rev).astype(BF16)


def _seqdft(pq, cos_t, sin_t, rev, tk):
    b, s, w = pq.shape
    half = w // 2
    nk = s // 2 // tk
    rows = cos_t.shape[1]
    kern = functools.partial(_seqdft_kernel, scale=1.0 / math.sqrt(s * LANES))
    out = jax.ShapeDtypeStruct((b, s // 2, half), BF16)
    return pl.pallas_call(
        kern,
        grid=(b, nk),
        in_specs=[pl.BlockSpec((1, rows, s), lambda i, k: (k, 0, 0)),
                  pl.BlockSpec((1, rows, s), lambda i, k: (k, 0, 0)),
                  pl.BlockSpec((1, s, w), lambda i, k: (i, 0, 0), pipeline_mode=pl.Buffered(1)),
                  _const_spec(rev.shape)],
        out_specs=[pl.BlockSpec((1, tk, half), lambda i, k: (i, k, 0)),
                   pl.BlockSpec((1, tk, half), lambda i, k: (i, nk - 1 - k, 0))],
        out_shape=[out, out],
        compiler_params=pltpu.CompilerParams(
            dimension_semantics=("parallel", "arbitrary"), vmem_limit_bytes=VMEM_LIMIT),
        name="seqdft",
    )(cos_t, sin_t, pq, rev)


def _dft_tables(s, tk):
    nk = s // 2 // tk
    rows = tk + SUBLANES
    t = jnp.arange(s, dtype=jnp.int32)
    k0 = jnp.arange(nk, dtype=jnp.int32) * tk
    ki = jnp.arange(rows, dtype=jnp.int32)
    w = 2.0 * math.pi / s
    ang_a = ((k0[:, None] * t[None, :]) % s).astype(F32) * w
    ang_b = ((ki[:, None] * t[None, :]) % s).astype(F32) * w
    ca, sa = jnp.cos(ang_a)[:, None, :], jnp.sin(ang_a)[:, None, :]
    cb, sb = jnp.cos(ang_b)[None], jnp.sin(ang_b)[None]
    cos_t = (ca * cb - sa * sb).astype(BF16)
    sin_t = (sa * cb + ca * sb).astype(BF16)
    m = jnp.arange(tk, dtype=jnp.int32)
    rev = (m[:, None] + m[None, :] == tk).astype(BF16)
    return cos_t, sin_t, rev


def _merge_kernel(hs_ref, yag_ref, flo_ref, fhi_ref, x_ref, mod_ref, gpost_ref, gpre_ref,
                  wa_ref, wb_ref, wo_ref, x1_ref, h2_ref):
    d = x_ref.shape[-1]
    m = mod_ref[0]
    lower_half = pl.program_id(1) < pl.num_programs(1) // 2
    fo = jnp.where(lower_half, flo_ref[0], fhi_ref[0])
    hsum = jnp.concatenate([hs_ref[0, hh] for hh in range(N_HEADS)], axis=1)
    ya = yag_ref[0, :, 0:d].astype(F32)
    hy = hsum * ya
    z = (hy + hy * _gelu_tanh(ya)).astype(BF16)
    y_a = jnp.dot(z, wa_ref[...], preferred_element_type=F32)
    y_b = jnp.dot(fo, wb_ref[...], preferred_element_type=F32)
    ta = jnp.tanh(yag_ref[0, :, d:2 * d].astype(F32))
    tb = jnp.tanh(yag_ref[0, :, 2 * d:3 * d].astype(F32))
    merged = (1.0 + ta) * y_a + (1.0 + tb) * y_b
    mo = jnp.dot(merged.astype(BF16), wo_ref[...], preferred_element_type=F32)
    x1 = x_ref[0] + _rms_scaled(mo, gpost_ref[...] * m[2:3])
    x1_ref[0] = x1
    h2 = _rms_scaled(x1, gpre_ref[...] * (1.0 + m[4:5])) + m[3:4]
    h2_ref[0] = h2.astype(BF16)


def _merge(hs, yag, f_lo, f_hi, x, mod, g_post, g_pre2, wa, wb, wo, ts):
    b, s, d = x.shape
    tok = lambda i, j: (i, j, 0)
    nh = s // 2 // ts
    return pl.pallas_call(
        _merge_kernel,
        grid=(b, s // ts),
        in_specs=[pl.BlockSpec((1, N_HEADS, ts, LANES), lambda i, j: (i, 0, j, 0)),
                  pl.BlockSpec((1, ts, 3 * d), tok),
                  pl.BlockSpec((1, ts, f_lo.shape[-1]), lambda i, j: (i, jnp.minimum(j, nh - 1), 0)),
                  pl.BlockSpec((1, ts, f_hi.shape[-1]), lambda i, j: (i, jnp.maximum(j - nh, 0), 0)),
                  pl.BlockSpec((1, ts, d), tok),
                  pl.BlockSpec((1, N_MOD, d), lambda i, j: (i, 0, 0)),
                  _const_spec((1, d)), _const_spec((1, d)),
                  _const_spec(wa.shape), _const_spec(wb.shape), _const_spec(wo.shape)],
        out_specs=[pl.BlockSpec((1, ts, d), tok), pl.BlockSpec((1, ts, d), tok)],
        out_shape=[jax.ShapeDtypeStruct((b, s, d), F32), jax.ShapeDtypeStruct((b, s, d), BF16)],
        compiler_params=pltpu.CompilerParams(
            dimension_semantics=("parallel", "parallel"), vmem_limit_bytes=VMEM_LIMIT),
        name="merge",
    )(hs, yag, f_lo, f_hi, x, mod, g_post, g_pre2, wa, wb, wo)


def _ffn_kernel(h2_ref, prev_ref, next_ref, x1_ref, mod_ref, gpost_ref,
                wup_ref, cw_ref, cb_ref, wdn_ref, o_ref, hcat_ref, up0_ref, up1_ref, acc_ref):
    ts = h2_ref.shape[1]
    i = pl.program_id(1)
    last = pl.num_programs(1) - 1
    halo = BF16_ROWS
    hcat_ref[0:halo, :] = jnp.where(i > 0, prev_ref[0], jnp.zeros_like(prev_ref[0]))
    hcat_ref[halo:halo + ts, :] = h2_ref[0]
    hcat_ref[halo + ts:, :] = jnp.where(i < last, next_ref[0], jnp.zeros_like(next_ref[0]))
    n_chunks = wup_ref.shape[0]
    up_bufs = (up0_ref, up1_ref)
    n_slabs = up0_ref.shape[0]
    half = n_slabs // 2

    def up_proj(c, buf):
        up = jnp.dot(hcat_ref[...], wup_ref[c], preferred_element_type=F32)
        for k in range(n_slabs):
            up_bufs[buf][k] = up[:, k * LANES:(k + 1) * LANES]

    def conv_down(c, buf, first=False):
        w = cw_ref[c]
        cb = cb_ref[c]
        conv = []
        for k in range(n_slabs):
            lanes = slice(k * LANES, (k + 1) * LANES)
            before = up_bufs[buf][k, pl.ds(halo - 1, ts, stride=1), :]
            here = up_bufs[buf][k, pl.ds(halo, ts), :]
            after = up_bufs[buf][k, pl.ds(halo + 1, ts, stride=1), :]
            conv.append(cb[:, lanes] + w[0:1, lanes] * before + w[1:2, lanes] * here
                        + w[2:3, lanes] * after)
        gv = [conv[half + k] * conv[k] for k in range(half)]
        act = jnp.concatenate([gv[k] + gv[k] * _gelu_tanh(conv[half + k]) for k in range(half)], axis=1)
        part = jnp.dot(act.astype(BF16), wdn_ref[c], preferred_element_type=F32)
        if first:
            acc_ref[...] = part
        else:
            acc_ref[...] += part

    up_proj(0, 0)
    up_proj(1, 1)
    conv_down(0, 0, first=True)

    def pair(p, carry):
        c = 2 * p + 1
        up_proj(c + 1, 0)
        conv_down(c, 1)
        up_proj(c + 2, 1)
        conv_down(c + 1, 0)
        return carry

    n_pairs = (n_chunks - 2) // 2
    lax.fori_loop(0, n_pairs, pair, 0)
    for c in range(2 * n_pairs + 1, n_chunks):
        if c + 1 < n_chunks:
            up_proj(c + 1, (c + 1) % 2)
        conv_down(c, c % 2)
    m = mod_ref[0]
    o_ref[0] = x1_ref[0] + _rms_scaled(acc_ref[...], gpost_ref[...] * m[5:6])


def _ffn(h2, x1, mod, g_post, wup_r, cw_r, cb_r, wdn_r, ts):
    b, s, d = x1.shape
    tok = lambda i, j: (i, j, 0)
    hb = ts // BF16_ROWS
    nblk = s // BF16_ROWS
    return pl.pallas_call(
        _ffn_kernel,
        grid=(b, s // ts),
        in_specs=[pl.BlockSpec((1, ts, d), tok),
                  pl.BlockSpec((1, BF16_ROWS, d), lambda i, j: (i, jnp.maximum(j * hb - 1, 0), 0)),
                  pl.BlockSpec((1, BF16_ROWS, d), lambda i, j: (i, jnp.minimum((j + 1) * hb, nblk - 1), 0)),
                  pl.BlockSpec((1, ts, d), tok),
                  pl.BlockSpec((1, N_MOD, d), lambda i, j: (i, 0, 0)),
                  _const_spec((1, d)),
                  _const_spec(wup_r.shape), _const_spec(cw_r.shape), _const_spec(cb_r.shape),
                  _const_spec(wdn_r.shape)],
        out_specs=pl.BlockSpec((1, ts, d), tok),
        out_shape=jax.ShapeDtypeStruct((b, s, d), F32),
        scratch_shapes=[pltpu.VMEM((ts + 2 * BF16_ROWS, d), BF16),
                        pltpu.VMEM((wup_r.shape[-1] // LANES, ts + 2 * BF16_ROWS, LANES), F32),
                        pltpu.VMEM((wup_r.shape[-1] // LANES, ts + 2 * BF16_ROWS, LANES), F32),
                        pltpu.VMEM((ts, d), F32)],
        compiler_params=pltpu.CompilerParams(
            dimension_semantics=("parallel", "arbitrary"), vmem_limit_bytes=VMEM_LIMIT),
        name="ffn",
    )(h2, h2, h2, x1, mod, g_post, wup_r, cw_r, cb_r, wdn_r)


def _prepare_params(g_pre_mix, w_in, conv_w, conv_b, w_lru_a, b_lru_a, w_lru_x, b_lru_x, lru_lambda,
                    w_a_out, w_b_out, w_o, g_post_mix, g_pre_ffn, w_up, ffn_conv_w, ffn_conv_b,
                    w_down, g_post_ffn):
    d = w_in.shape[0]
    d_ff = w_down.shape[0]
    d_f = w_b_out.shape[0]
    w_in_r = jnp.concatenate([w_in[:, :2 * d], 0.5 * w_in[:, 2 * d + d_f:], w_in[:, 2 * d:2 * d + d_f]],
                             axis=1).astype(BF16)
    ang = (2.0 * math.pi / LANES) * ((jnp.arange(LANES, dtype=jnp.int32)[:, None]
                                      * jnp.arange(LANES, dtype=jnp.int32)[None, :]) % LANES).astype(F32)
    dft_g = jnp.concatenate([jnp.cos(ang), jnp.sin(ang)], axis=1).astype(BF16)
    wg = jnp.concatenate([w_lru_a, w_lru_x], axis=-1).astype(BF16)
    bg = jnp.concatenate([b_lru_a.reshape(2, N_HEADS, 1, LANES),
                          b_lru_x.reshape(2, N_HEADS, 1, LANES)], axis=-1)
    lam = lru_lambda.reshape(2, N_HEADS, 1, LANES)
    nc = d_ff // FFN_CHUNK
    val_w = w_up[:, :d_ff].reshape(d, nc, FFN_CHUNK)
    gate_w = w_up[:, d_ff:].reshape(d, nc, FFN_CHUNK)
    wup_r = jnp.concatenate([val_w, gate_w], axis=-1).transpose(1, 0, 2).astype(BF16)
    cw_r = jnp.concatenate([ffn_conv_w[:, :d_ff].reshape(3, nc, FFN_CHUNK),
                            ffn_conv_w[:, d_ff:].reshape(3, nc, FFN_CHUNK)], axis=-1).transpose(1, 0, 2)
    cb_r = jnp.concatenate([ffn_conv_b[:d_ff].reshape(nc, 1, FFN_CHUNK),
                            ffn_conv_b[d_ff:].reshape(nc, 1, FFN_CHUNK)], axis=-1)
    wdn_r = (0.5 * w_down).reshape(nc, FFN_CHUNK, d).astype(BF16)
    return dict(g_pre_mix=g_pre_mix.reshape(1, d), w_in_r=w_in_r, dft_g=dft_g,
                conv_w_half=0.5 * conv_w, conv_b_half=0.5 * conv_b.reshape(1, d), wg=wg, bg_half=0.5 * bg,
                lam=lam,
                wa=(0.5 * w_a_out).astype(BF16), wb=w_b_out.astype(BF16), wo=(0.5 * w_o).astype(BF16),
                g_post_mix=g_post_mix.reshape(1, d), g_pre_ffn=g_pre_ffn.reshape(1, d),
                wup_r=wup_r, cw_r=cw_r, cb_r=cb_r, wdn_r=wdn_r, g_post_ffn=g_post_ffn.reshape(1, d))


def _encoder_layer(x, mod, p, ts, tk):
    s = x.shape[1]
    xa, yag, pq = _inproj(x, mod, p["g_pre_mix"], p["w_in_r"], p["dft_g"], ts)
    hs = _lru(xa, p["conv_w_half"], p["conv_b_half"], p["wg"], p["bg_half"], p["lam"])
    f_lo, f_hi = _seqdft(pq, *_dft_tables(s, tk), tk)
    x1, h2 = _merge(hs, yag, f_lo, f_hi, x, mod, p["g_post_mix"], p["g_pre_ffn"], p["wa"], p["wb"], p["wo"], ts)
    return _ffn(h2, x1, mod, p["g_post_ffn"], p["wup_r"], p["cw_r"], p["cb_r"], p["wdn_r"], ts)


def _seqdft_tile(s):
    return max(SUBLANES * 16, min(512, (2 * 1024 * 1024) // s))


def kernel(x_prompt, x_sample, c_prompt, c_sample, w_ada, b_ada, g_pre_mix, w_in, conv_w, conv_b, w_lru_a, b_lru_a, w_lru_x, b_lru_x, lru_lambda, w_a_out, w_b_out, w_o, g_post_mix, g_pre_ffn, w_up, ffn_conv_w, ffn_conv_b, w_down, g_post_ffn):
    depth = w_ada.shape[0]
    nb_p = x_prompt.shape[0]
    d = x_prompt.shape[-1]
    y_p, y_s = x_prompt, x_sample
    c_all = jnp.concatenate([c_prompt, c_sample], axis=0)
    for l in range(depth):
        p = _prepare_params(g_pre_mix[l], w_in[l], conv_w[l], conv_b[l], w_lru_a[l], b_lru_a[l],
                            w_lru_x[l], b_lru_x[l], lru_lambda[l], w_a_out[l], w_b_out[l], w_o[l],
                            g_post_mix[l], g_pre_ffn[l], w_up[l], ffn_conv_w[l], ffn_conv_b[l],
                            w_down[l], g_post_ffn[l])
        mod = _modulation(c_all, w_ada[l], b_ada[l]).reshape(-1, N_MOD, d)
        y_p = _encoder_layer(y_p, mod[:nb_p], p, 512, _seqdft_tile(y_p.shape[1]))
        y_s = _encoder_layer(y_s, mod[nb_p:], p, 512, _seqdft_tile(y_s.shape[1]))
    return (y_p, y_s)
```
